```python
import math
import jax, jax.numpy as jnp
from jax import lax
import numpy as np

D_MODEL = 2048
BATCH = 2
SEQ = 16384
DEPTH = 1

CHUNK = 64
Q_BLOCK = 128
EPS = 1e-6
D_SSM = D_MODEL // 2
SSM_GROUP = 16
N_SSM_GROUPS = D_SSM // SSM_GROUP
SSM_STATE = 64
DT_MIN = 0.001
DT_MAX = 0.1
N_MLA_HEADS = 8
QK_NOPE = 128
QK_ROPE = 64
V_HEAD = 128
D_MLA = N_MLA_HEADS * V_HEAD
Q_LORA = 512
KV_LORA = 256
ROPE_BASE = 10000.0
D_IN = 2 * D_SSM + Q_LORA + KV_LORA + QK_ROPE
D_MIX = D_SSM + D_MLA
N_EXPERTS = 32
TOP_K = 4
D_FF = D_MODEL
SWIGLU_LIMIT = 7.0
SWIGLU_ALPHA = 1.702
MOE_BLOCK = 256
N_MOD = 6

kernel_name = 'hybrid_s5_mla_moe_adaln_block'


def rms_norm(x, g):
    xf = x.astype(jnp.float32)
    y = xf * lax.rsqrt(jnp.mean(xf * xf, axis=-1, keepdims=True) + EPS)
    return (y * g.astype(jnp.float32)).astype(x.dtype)


def rope(x, cos, sin):
    half = x.shape[-1] // 2
    x1, x2 = x[..., :half], x[..., half:]
    c, s = cos.astype(x.dtype), sin.astype(x.dtype)
    return jnp.concatenate([x1 * c - x2 * s, x2 * c + x1 * s], axis=-1)


def s5_mixer(u, A_re, A_im, B_re, B_im, C_re, C_im, D, log_dt):
    Bsz, L, _ = u.shape
    f32 = jnp.float32
    G, P, H = N_SSM_GROUPS, SSM_STATE, SSM_GROUP
    A = lax.complex(A_re.astype(f32), A_im.astype(f32))
    dtA = A * jnp.exp(log_dt.astype(f32))[:, None]
    A_bar = jnp.exp(dtA)
    Bc = lax.complex(B_re.astype(f32), B_im.astype(f32))
    B_bar = ((A_bar - 1.0) / A)[..., None] * Bc
    Cc = lax.complex(C_re.astype(f32), C_im.astype(f32))
    powers = jnp.exp(jnp.arange(1, CHUNK + 1, dtype=f32)[:, None, None] * dtA[None])
    a_chunk = jnp.broadcast_to(A_bar, (Bsz, CHUNK, G, P))

    def combine(left, right):
        a_l, b_l = left
        a_r, b_r = right
        return a_r * a_l, a_r * b_l + b_r

    def step(state, u_c):
        bu = jnp.einsum('gph,blgh->blgp', B_bar, u_c.astype(jnp.complex64))
        _, local = lax.associative_scan(combine, (a_chunk, bu), axis=1)
        states = local + powers[None] * state[:, None]
        y = jnp.einsum('ghp,blgp->blgh', Cc, states).real
        return states[:, -1], y

    u_chunks = jnp.moveaxis(u.astype(f32).reshape(Bsz, L // CHUNK, CHUNK, G, H), 1, 0)
    state0 = jnp.zeros((Bsz, G, P), jnp.complex64)
    _, ys = lax.scan(step, state0, u_chunks)
    y = jnp.moveaxis(ys, 0, 1).reshape(Bsz, L, D_SSM)
    return y + D.astype(f32) * u.astype(f32)


def mla_mixer(q_lat, kv_lat, k_rope, cos, sin, q_lat_g, kv_lat_g, w_uq, w_ukv,
              q_nope_g, q_rope_g, k_nope_g, k_rope_g):
    Bsz, L, _ = q_lat.shape
    H = N_MLA_HEADS
    q = (rms_norm(q_lat, q_lat_g) @ w_uq).reshape(Bsz, L, H, QK_NOPE + QK_ROPE)
    kv = (rms_norm(kv_lat, kv_lat_g) @ w_ukv).reshape(Bsz, L, H, QK_NOPE + V_HEAD)
    q_nope = rms_norm(q[..., :QK_NOPE], q_nope_g)
    q_rp = rope(rms_norm(q[..., QK_NOPE:], q_rope_g), cos[:, :, None], sin[:, :, None])
    k_nope = rms_norm(kv[..., :QK_NOPE], k_nope_g)
    v = kv[..., QK_NOPE:]
    k_rp = rope(rms_norm(k_rope, k_rope_g), cos, sin)
    scale = (QK_NOPE + QK_ROPE) ** -0.5
    n_blk = L // Q_BLOCK
    qn_b = jnp.swapaxes(q_nope.reshape(Bsz, n_blk, Q_BLOCK, H, QK_NOPE), 0, 1)
    qr_b = jnp.swapaxes(q_rp.reshape(Bsz, n_blk, Q_BLOCK, H, QK_ROPE), 0, 1)
    key_chunk = jnp.arange(L) // CHUNK

    def attend(args):
        blk, qn, qr = args
        s = (jnp.einsum('bqhd,bkhd->bhqk', qn, k_nope, preferred_element_type=jnp.float32)
             + jnp.einsum('bqhr,bkr->bhqk', qr, k_rp, preferred_element_type=jnp.float32)) * scale
        q_chunk = (blk * Q_BLOCK + jnp.arange(Q_BLOCK)) // CHUNK
        mask = key_chunk[None, :] <= q_chunk[:, None]
        s = jnp.where(mask[None, None], s, -jnp.inf)
        p = jax.nn.softmax(s, axis=-1)
        return jnp.einsum('bhqk,bkhd->bqhd', p.astype(v.dtype), v)

    out = lax.map(attend, (jnp.arange(n_blk), qn_b, qr_b))
    return jnp.swapaxes(out, 0, 1).reshape(Bsz, L, D_MLA)


def moe_ffn(h, w_router, b_router, w_gate_up, b_gate_up, w_down, b_down):
    Bsz, L, D = h.shape
    T = Bsz * L
    E, K, M = N_EXPERTS, TOP_K, MOE_BLOCK
    xt = h.reshape(T, D)
    logits = (xt @ w_router).astype(jnp.float32) + b_router.astype(jnp.float32)
    top_val, top_idx = lax.top_k(logits, K)
    gates = jax.nn.softmax(top_val, axis=-1)
    flat_e = top_idx.reshape(-1)
    flat_tok = jnp.arange(T * K, dtype=jnp.int32) // K
    flat_gate = gates.reshape(-1)
    order = jnp.argsort(flat_e)
    sorted_e = flat_e[order]
    counts = jnp.bincount(flat_e, length=E)
    starts = jnp.cumsum(counts) - counts
    padded = ((counts + M - 1) // M) * M
    pad_ends = jnp.cumsum(padded)
    pad_starts = pad_ends - padded
    dest = pad_starts[sorted_e] + (jnp.arange(T * K) - starts[sorted_e])
    n_blocks = -(-(T * K) // M) + E
    n_rows = n_blocks * M
    row_tok = jnp.zeros((n_rows,), jnp.int32).at[dest].set(flat_tok[order])
    row_gate = jnp.zeros((n_rows,), jnp.float32).at[dest].set(flat_gate[order])
    block_e = jnp.minimum(jnp.searchsorted(pad_ends, jnp.arange(n_blocks) * M, side='right'), E - 1)

    def expert_block(y, args):
        e, tok, g = args
        xb = xt[tok]
        gu = xb @ w_gate_up[e] + b_gate_up[e]
        gate = jnp.minimum(gu[:, 0::2], SWIGLU_LIMIT)
        up = jnp.clip(gu[:, 1::2], -SWIGLU_LIMIT, SWIGLU_LIMIT)
        act = (up + 1.0) * (gate * jax.nn.sigmoid(SWIGLU_ALPHA * gate))
        yb = (act @ w_down[e] + b_down[e]) * g[:, None].astype(xt.dtype)
        return y.at[tok].add(yb), None

    y0 = jnp.zeros((T, D), xt.dtype)
    y, _ = lax.scan(expert_block, y0,
                    (block_e, row_tok.reshape(n_blocks, M), row_gate.reshape(n_blocks, M)))
    return y.reshape(Bsz, L, D)


def setup_inputs(seed: int = 0) -> dict:
    key = jax.random.key(seed)
    ks = jax.random.split(key, 40)
    f32 = jnp.float32
    nrm = lambda k, shape, s: jax.random.normal(k, shape, f32) * s
    gain = lambda k, shape: 1.0 + 0.02 * jax.random.normal(k, shape, f32)
    G, P, H = N_SSM_GROUPS, SSM_STATE, SSM_GROUP
    offsets = jax.random.randint(ks[2], (BATCH, 1), 0, 1024, dtype=jnp.int32) * CHUNK
    positions = offsets + jnp.arange(SEQ, dtype=jnp.int32)[None, :]
    A_im = math.pi * jnp.arange(P, dtype=f32)[None, None, :] + nrm(ks[7], (DEPTH, G, P), 0.01)
    log_dt = jax.random.uniform(ks[14], (DEPTH, G), f32, math.log(DT_MIN), math.log(DT_MAX))
    return {
        'x': nrm(ks[0], (BATCH, SEQ, D_MODEL), 1.0),
        'c': nrm(ks[1], (BATCH, D_MODEL), 1.0),
        'positions': positions,
        'w_ada': nrm(ks[3], (DEPTH, D_MODEL, N_MOD * D_MODEL), 0.5 * D_MODEL ** -0.5),
        'b_ada': nrm(ks[4], (DEPTH, N_MOD * D_MODEL), 0.01),
        'norm_mix_g': gain(ks[5], (DEPTH, D_MODEL)),
        'w_in': nrm(ks[6], (DEPTH, D_MODEL, D_IN), D_MODEL ** -0.5),
        'ssm_A_re': -0.5 + nrm(ks[8], (DEPTH, G, P), 0.01),
        'ssm_A_im': A_im,
        'ssm_B_re': nrm(ks[9], (DEPTH, G, P, H), (2 * H) ** -0.5),
        'ssm_B_im': nrm(ks[10], (DEPTH, G, P, H), (2 * H) ** -0.5),
        'ssm_C_re': nrm(ks[11], (DEPTH, G, H, P), (2 * P) ** -0.5),
        'ssm_C_im': nrm(ks[12], (DEPTH, G, H, P), (2 * P) ** -0.5),
        'ssm_D': nrm(ks[13], (DEPTH, D_SSM), 1.0),
        'ssm_log_dt': log_dt,
        'q_lat_g': gain(ks[15], (DEPTH, Q_LORA)),
        'kv_lat_g': gain(ks[16], (DEPTH, KV_LORA)),
        'w_uq': nrm(ks[17], (DEPTH, Q_LORA, N_MLA_HEADS * (QK_NOPE + QK_ROPE)), Q_LORA ** -0.5),
        'w_ukv': nrm(ks[18], (DEPTH, KV_LORA, N_MLA_HEADS * (QK_NOPE + V_HEAD)), KV_LORA ** -0.5),
        'q_nope_g': gain(ks[19], (DEPTH, QK_NOPE)),
        'q_rope_g': gain(ks[20], (DEPTH, QK_ROPE)),
        'k_nope_g': gain(ks[21], (DEPTH, QK_NOPE)),
        'k_rope_g': gain(ks[22], (DEPTH, QK_ROPE)),
        'out_ssm_g': gain(ks[23], (DEPTH, D_SSM)),
        'out_mla_g': gain(ks[24], (DEPTH, D_MLA)),
        'w_out': nrm(ks[25], (DEPTH, D_MIX, D_MODEL), D_MIX ** -0.5),
        'norm_ffn_g': gain(ks[26], (DEPTH, D_MODEL)),
        'w_router': nrm(ks[27], (DEPTH, D_MODEL, N_EXPERTS), D_MODEL ** -0.5),
        'b_router': nrm(ks[28], (DEPTH, N_EXPERTS), 0.01),
        'w_gate_up': nrm(ks[29], (DEPTH, N_EXPERTS, D_MODEL, 2 * D_FF), D_MODEL ** -0.5),
        'b_gate_up': nrm(ks[30], (DEPTH, N_EXPERTS, 2 * D_FF), 0.01),
        'w_down': nrm(ks[31], (DEPTH, N_EXPERTS, D_FF, D_MODEL), D_FF ** -0.5),
        'b_down': nrm(ks[32], (DEPTH, N_EXPERTS, D_MODEL), 0.01),
    }


def reference(x, c, positions, w_ada, b_ada, norm_mix_g, w_in, ssm_A_re, ssm_A_im, ssm_B_re,
              ssm_B_im, ssm_C_re, ssm_C_im, ssm_D, ssm_log_dt, q_lat_g, kv_lat_g, w_uq, w_ukv,
              q_nope_g, q_rope_g, k_nope_g, k_rope_g, out_ssm_g, out_mla_g, w_out, norm_ffn_g,
              w_router, b_router, w_gate_up, b_gate_up, w_down, b_down):
    inv_freq = ROPE_BASE ** (-jnp.arange(0, QK_ROPE, 2, dtype=jnp.float32) / QK_ROPE)
    ang = positions.astype(jnp.float32)[..., None] * inv_freq
    cos, sin = jnp.cos(ang), jnp.sin(ang)
    c_act = jax.nn.silu(c)
    s1, s2, s3 = 2 * D_SSM, 2 * D_SSM + Q_LORA, 2 * D_SSM + Q_LORA + KV_LORA
    for l in range(DEPTH):
        mod = c_act @ w_ada[l] + b_ada[l]
        sh1, sc1, g1, sh2, sc2, g2 = [m[:, None, :] for m in jnp.split(mod, N_MOD, axis=-1)]
        h = rms_norm(x, norm_mix_g[l]) * (1.0 + sc1) + sh1
        z = h @ w_in[l]
        u, g_ssm = z[..., :D_SSM], z[..., D_SSM:s1]
        q_lat, kv_lat, k_rope = z[..., s1:s2], z[..., s2:s3], z[..., s3:]
        y_s5 = s5_mixer(u, ssm_A_re[l], ssm_A_im[l], ssm_B_re[l], ssm_B_im[l],
                        ssm_C_re[l], ssm_C_im[l], ssm_D[l], ssm_log_dt[l])
        y_s5 = (jax.nn.gelu(y_s5) * jax.nn.sigmoid(g_ssm.astype(jnp.float32))).astype(x.dtype)
        y_mla = mla_mixer(q_lat, kv_lat, k_rope, cos, sin, q_lat_g[l], kv_lat_g[l], w_uq[l],
                          w_ukv[l], q_nope_g[l], q_rope_g[l], k_nope_g[l], k_rope_g[l])
        mix = jnp.concatenate([rms_norm(y_s5, out_ssm_g[l]), rms_norm(y_mla, out_mla_g[l])], axis=-1)
        x = x + g1 * (mix @ w_out[l])
        h2 = rms_norm(x, norm_ffn_g[l]) * (1.0 + sc2) + sh2
        x = x + g2 * moe_ffn(h2, w_router[l], b_router[l], w_gate_up[l], b_gate_up[l],
                             w_down[l], b_down[l])
    return x
```

```python
import functools
import math

import jax
import jax.numpy as jnp
from jax import lax
from jax.experimental import pallas as pl
from jax.experimental.pallas import tpu as pltpu

F32 = jnp.float32
BF16 = jnp.bfloat16
I32 = jnp.int32

D_MODEL = 2048
CHUNK = 64
EPS = 1e-6
D_SSM = 1024
SSM_H = 16
SSM_G = D_SSM // SSM_H
SSM_P = 64
SSM_LC = 64
SSM_SB = 16
SSM_NSB = SSM_LC // SSM_SB
SSM_W = SSM_LC * SSM_H
SSM_BW = SSM_SB * SSM_H
N_HEADS = 8
QK_NOPE = 128
QK_ROPE = 64
V_HEAD = 128
D_MLA = N_HEADS * V_HEAD
Q_LORA = 512
KV_LORA = 256
QK_DIM = QK_NOPE + QK_ROPE
ROPE_BASE = 10000.0
D_IN = 2 * D_SSM + Q_LORA + KV_LORA + QK_ROPE
N_EXPERTS = 32
TOP_K = 4
D_FF = D_MODEL
SWIGLU_LIMIT = 7.0
SWIGLU_ALPHA = 1.702
MOE_BLOCK = 256
N_MOD = 6

VMEM_LIMIT = 56 * 1024 * 1024

ROW_TILE = 256
ATTN_BLOCK = 1024
ROUTE_TILE = 512
DISPATCH_TILE = 256
COMBINE_TILE = 128
FF_TILE = 512


def _params(sem, limit=VMEM_LIMIT):
    return pltpu.CompilerParams(dimension_semantics=sem, vmem_limit_bytes=limit)


def _ada_kernel(c_ref, w_ref, b_ref, o_ref):
    c = c_ref[...]
    ca = c * jax.nn.sigmoid(c)
    o_ref[...] = jnp.dot(ca, w_ref[...], preferred_element_type=F32) + b_ref[...]


def _ada(c_pad, w_ada, b_ada):
    n = w_ada.shape[1]
    tn = 1024
    return pl.pallas_call(
        _ada_kernel,
        grid=(n // tn,),
        in_specs=[
            pl.BlockSpec((8, D_MODEL), lambda j: (0, 0)),
            pl.BlockSpec((D_MODEL, tn), lambda j: (0, j)),
            pl.BlockSpec((1, tn), lambda j: (0, j)),
        ],
        out_specs=pl.BlockSpec((8, tn), lambda j: (0, j)),
        out_shape=jax.ShapeDtypeStruct((8, n), F32),
        compiler_params=_params(("arbitrary",)),
        name="ada",
    )(c_pad, w_ada, b_ada)


def _inproj_kernel(x_ref, mod_ref, g_ref, w_ref, u_ref, gs_ref, ql_ref, kvl_ref, kr_ref):
    x = x_ref[0]
    sh = mod_ref[0, 0:1, :]
    sc = mod_ref[0, 1:2, :]
    h = x * lax.rsqrt(jnp.mean(x * x, axis=-1, keepdims=True) + EPS) * g_ref[...]
    h = h * (1.0 + sc) + sh
    z = jnp.dot(h.astype(BF16), w_ref[...], preferred_element_type=F32)
    s1, s2, s3 = 2 * D_SSM, 2 * D_SSM + Q_LORA, 2 * D_SSM + Q_LORA + KV_LORA
    u_ref[0] = z[:, :D_SSM].astype(BF16)
    gs_ref[0] = z[:, D_SSM:s1].astype(BF16)
    ql_ref[0] = z[:, s1:s2].astype(BF16)
    kvl_ref[0] = z[:, s2:s3].astype(BF16)
    kr_ref[0] = z[:, s3:].astype(BF16)


def _inproj(x, mod, g, w_in):
    b, l, d = x.shape
    tm = min(ROW_TILE, l)
    widths = (D_SSM, D_SSM, Q_LORA, KV_LORA, QK_ROPE)
    return pl.pallas_call(
        _inproj_kernel,
        grid=(b, l // tm),
        in_specs=[
            pl.BlockSpec((1, tm, d), lambda i, j: (i, j, 0)),
            pl.BlockSpec((1, N_MOD, d), lambda i, j: (i, 0, 0)),
            pl.BlockSpec((1, d), lambda i, j: (0, 0)),
            pl.BlockSpec((d, D_IN), lambda i, j: (0, 0)),
        ],
        out_specs=[pl.BlockSpec((1, tm, w), lambda i, j: (i, j, 0)) for w in widths],
        out_shape=[jax.ShapeDtypeStruct((b, l, w), BF16) for w in widths],
        compiler_params=_params(("parallel", "parallel")),
        name="inproj",
    )(x, mod, g, w_in)


def _ssm_tables(a_re, a_im, b_re, b_im, c_re, c_im, d_skip, log_dt):
    g, p, h = SSM_G, SSM_P, SSM_H
    a = lax.complex(a_re.astype(F32), a_im.astype(F32))
    dta = a * jnp.exp(log_dt.astype(F32))[:, None]
    a_bar = jnp.exp(dta)
    b_bar = ((a_bar - 1.0) / a)[..., None] * lax.complex(b_re.astype(F32), b_im.astype(F32))
    cc = lax.complex(c_re.astype(F32), c_im.astype(F32))
    pw = jnp.exp(jnp.arange(SSM_LC + 1, dtype=F32)[:, None, None] * dta[None])
    kern = jnp.einsum("ghp,jgp,gpk->gjhk", cc, pw[:SSM_LC], b_bar).real
    eye = jnp.eye(h, dtype=F32) * d_skip.astype(F32).reshape(g, 1, h)
    kern = kern.at[:, 0].add(eye)
    dd = jnp.arange(SSM_NSB)[:, None, None]
    ss = jnp.arange(SSM_SB)[None, :, None]
    tt = jnp.arange(SSM_SB)[None, None, :]
    lag = SSM_SB * dd + tt - ss
    kg = kern[:, jnp.clip(lag, 0, SSM_LC - 1)]
    kg = jnp.where((lag >= 0)[None, :, :, :, None, None], kg, 0.0)
    dblk = kg.transpose(0, 1, 2, 5, 3, 4).reshape(g, SSM_NSB, SSM_BW, SSM_BW).astype(BF16)

    wb = pw[SSM_LC - 1 - jnp.arange(SSM_LC)][:, :, :, None] * b_bar[None]
    wb = wb.transpose(1, 0, 3, 2).reshape(g, SSM_W, p)
    wb_re = wb.real.reshape(g // 2, 2, SSM_W, p)
    wb_im = wb.imag.reshape(g // 2, 2, SSM_W, p)
    z = jnp.zeros_like(wb_re[:, 0])
    wb_even = jnp.concatenate([wb_re[:, 0], z, wb_im[:, 0], z], axis=-1)
    wb_odd = jnp.concatenate([z, wb_re[:, 1], z, wb_im[:, 1]], axis=-1)
    wb_pair = jnp.stack([wb_even, wb_odd], axis=1).astype(BF16)

    cp = cc[:, None, :, :] * pw[1:, :, None, :].transpose(1, 0, 2, 3)
    wc = cp.transpose(0, 3, 1, 2).reshape(g, p, SSM_W)
    wc_re = wc.real.reshape(g // 2, 2, p, SSM_W)
    wc_im = (-wc.imag).reshape(g // 2, 2, p, SSM_W)
    zc = jnp.zeros_like(wc_re[:, 0])
    wc_pair = jnp.concatenate([
        jnp.concatenate([wc_re[:, 0], zc], axis=-1),
        jnp.concatenate([zc, wc_re[:, 1]], axis=-1),
        jnp.concatenate([wc_im[:, 0], zc], axis=-1),
        jnp.concatenate([zc, wc_im[:, 1]], axis=-1),
    ], axis=1).astype(BF16)

    a_blk = jnp.stack([pw[SSM_LC].real.reshape(-1), pw[SSM_LC].imag.reshape(-1)])
    return dblk, wb_pair, wc_pair, a_blk


def _ssm_in_kernel(u_ref, wb_ref, vre_ref, vim_ref):
    v = (jnp.dot(u_ref[0], wb_ref[0, 0], preferred_element_type=F32)
         + jnp.dot(u_ref[1], wb_ref[0, 1], preferred_element_type=F32))
    vre_ref[...] = v[:, :128]
    vim_ref[...] = v[:, 128:]


def _ssm_in(u_g, wb_pair):
    g, r, _ = u_g.shape
    out = jax.ShapeDtypeStruct((r, g * SSM_P), F32)
    return pl.pallas_call(
        _ssm_in_kernel,
        grid=(g // 2,),
        in_specs=[
            pl.BlockSpec((2, r, SSM_W), lambda j: (j, 0, 0)),
            pl.BlockSpec((1, 2, SSM_W, 256), lambda j: (j, 0, 0, 0)),
        ],
        out_specs=[pl.BlockSpec((r, 128), lambda j: (0, j))] * 2,
        out_shape=[out, out],
        compiler_params=_params(("parallel",)),
        name="ssm_in",
    )(u_g, wb_pair)


def _ssm_scan_kernel(vre_ref, vim_ref, a_ref, xre_ref, xim_ref):
    ar = a_ref[0:1, :]
    ai = a_ref[1:2, :]
    b, nc, lt = vre_ref.shape

    def body(c, carry):
        sre, sim = carry
        xre_ref[:, pl.ds(c, 1), :] = sre
        xim_ref[:, pl.ds(c, 1), :] = sim
        vre = vre_ref[:, pl.ds(c, 1), :]
        vim = vim_ref[:, pl.ds(c, 1), :]
        return ar * sre - ai * sim + vre, ar * sim + ai * sre + vim

    zero = jnp.zeros((b, 1, lt), F32)
    lax.fori_loop(0, nc, body, (zero, zero))


def _ssm_scan(vre, vim, a_blk):
    b, nc, n = vre.shape
    lt = 1024
    spec = pl.BlockSpec((b, nc, lt), lambda j: (0, 0, j))
    out = jax.ShapeDtypeStruct((b, nc, n), F32)
    return pl.pallas_call(
        _ssm_scan_kernel,
        grid=(n // lt,),
        in_specs=[spec, spec, pl.BlockSpec((2, lt), lambda j: (0, j))],
        out_specs=[spec, spec],
        out_shape=[out, out],
        compiler_params=_params(("parallel",)),
        name="ssm_scan",
    )(vre, vim, a_blk)


def _ssm_out_kernel(u_ref, d_ref, xre_ref, xim_ref, wc_ref, y_ref):
    xcat = jnp.concatenate([xre_ref[...], xim_ref[...]], axis=1).astype(BF16)
    yx = jnp.dot(xcat, wc_ref[0], preferred_element_type=F32)
    for gi in range(2):
        for i in range(SSM_NSB):
            acc = yx[:, gi * SSM_W + i * SSM_BW: gi * SSM_W + (i + 1) * SSM_BW]
            for j in range(i + 1):
                acc = acc + jnp.dot(u_ref[gi, :, j * SSM_BW:(j + 1) * SSM_BW], d_ref[gi, i - j],
                                    preferred_element_type=F32)
            y_ref[gi, :, i * SSM_BW:(i + 1) * SSM_BW] = acc.astype(BF16)


def _ssm_out(u_g, dblk, xre, xim, wc_pair):
    g, r, _ = u_g.shape
    return pl.pallas_call(
        _ssm_out_kernel,
        grid=(g // 2,),
        in_specs=[
            pl.BlockSpec((2, r, SSM_W), lambda j: (j, 0, 0)),
            pl.BlockSpec((2, SSM_NSB, SSM_BW, SSM_BW), lambda j: (j, 0, 0, 0)),
            pl.BlockSpec((r, 128), lambda j: (0, j)),
            pl.BlockSpec((r, 128), lambda j: (0, j)),
            pl.BlockSpec((1, 256, 2 * SSM_W), lambda j: (j, 0, 0)),
        ],
        out_specs=pl.BlockSpec((2, r, SSM_W), lambda j: (j, 0, 0)),
        out_shape=jax.ShapeDtypeStruct((g, r, SSM_W), BF16),
        compiler_params=_params(("parallel",)),
        name="ssm_out",
    )(u_g, dblk, xre, xim, wc_pair)


def _s5(u, tables):
    dblk, wb_pair, wc_pair, a_blk = tables
    b, l, _ = u.shape
    nc = l // SSM_LC
    u_g = u.reshape(b, nc, SSM_LC, SSM_G, SSM_H).transpose(3, 0, 1, 2, 4).reshape(SSM_G, b * nc, SSM_W)
    vre, vim = _ssm_in(u_g, wb_pair)
    n = SSM_G * SSM_P
    xre, xim = _ssm_scan(vre.reshape(b, nc, n), vim.reshape(b, nc, n), a_blk)
    y_g = _ssm_out(u_g, dblk, xre.reshape(b * nc, n), xim.reshape(b * nc, n), wc_pair)
    return y_g.reshape(SSM_G, b, nc, SSM_LC, SSM_H).transpose(1, 2, 3, 0, 4).reshape(b, l, D_SSM)


def _rms(x, g, n):
    return x * lax.rsqrt(jnp.sum(x * x, axis=-1, keepdims=True) * (1.0 / n) + EPS) * g


def _mla_prep_kernel(ql_ref, kvl_ref, kr_ref, cos_ref, sin_ref, qlg_ref, kvlg_ref, qng_ref, qrg_ref,
                     kng_ref, krg_ref, wq_ref, wkv_ref, seg_ref, q_ref, k_ref, v_ref):
    scale = QK_DIM ** -0.5
    cs = cos_ref[0]
    sn = sin_ref[0]
    ql = _rms(ql_ref[0].astype(F32), qlg_ref[...], Q_LORA)
    q = jnp.dot(ql.astype(BF16), wq_ref[...], preferred_element_type=F32)
    kvl = _rms(kvl_ref[0].astype(F32), kvlg_ref[...], KV_LORA)
    kv = jnp.dot(kvl.astype(BF16), wkv_ref[...], preferred_element_type=F32)

    nr = N_HEADS * QK_ROPE
    qr = q[:, N_HEADS * QK_NOPE:]
    sq = qr * qr
    sq_hi = sq.astype(BF16)
    sq_lo = (sq - sq_hi.astype(F32)).astype(BF16)
    ssq = (jnp.dot(sq_hi, seg_ref[...], preferred_element_type=F32)
           + jnp.dot(sq_lo, seg_ref[...], preferred_element_type=F32))
    qr = qr * lax.rsqrt(ssq * (1.0 / QK_ROPE) + EPS) * qrg_ref[...]
    lane = lax.broadcasted_iota(I32, qr.shape, 1)
    first_half = (lane % QK_ROPE) < (QK_ROPE // 2)
    partner = jnp.where(first_half, pltpu.roll(qr, nr - QK_ROPE // 2, axis=1),
                        pltpu.roll(qr, QK_ROPE // 2, axis=1))
    cs4 = jnp.concatenate([cs] * (nr // 128), axis=1)
    sn4 = jnp.concatenate([sn] * (nr // 128), axis=1)
    qr = (qr * cs4 + partner * sn4) * scale

    kr = _rms(kr_ref[0].astype(F32), krg_ref[...], QK_ROPE)
    kr_partner = jnp.concatenate([kr[:, QK_ROPE // 2:], kr[:, :QK_ROPE // 2]], axis=1)
    kr = (kr * cs[:, :QK_ROPE] + kr_partner * sn[:, :QK_ROPE]).astype(BF16)

    for h in range(N_HEADS):
        qn = _rms(q[:, h * QK_NOPE:(h + 1) * QK_NOPE], qng_ref[...], QK_NOPE) * scale
        q_ref[0, h, :, :QK_NOPE] = qn.astype(BF16)
        q_ref[0, h, :, QK_NOPE:] = qr[:, h * QK_ROPE:(h + 1) * QK_ROPE].astype(BF16)
        kn = _rms(kv[:, h * QK_NOPE:(h + 1) * QK_NOPE], kng_ref[...], QK_NOPE)
        k_ref[0, h, :, :QK_NOPE] = kn.astype(BF16)
        k_ref[0, h, :, QK_NOPE:] = kr
        v_ref[0, h] = kv[:, N_HEADS * QK_NOPE + h * V_HEAD: N_HEADS * QK_NOPE + (h + 1) * V_HEAD].astype(BF16)


def _mla_prep(ql, kvl, kr, cos2, sin2, qlg, kvlg, qng, qrg, kng, krg, wq, wkv, seg):
    b, l, _ = ql.shape
    tm = min(ROW_TILE, l)
    row = lambda w: pl.BlockSpec((1, tm, w), lambda i, j: (i, j, 0))
    full = lambda a: pl.BlockSpec(a.shape, lambda i, j: (0,) * a.ndim)
    head = lambda w: pl.BlockSpec((1, N_HEADS, tm, w), lambda i, j: (i, 0, j, 0))
    consts = (qlg, kvlg, qng, qrg, kng, krg, wq, wkv, seg)
    return pl.pallas_call(
        _mla_prep_kernel,
        grid=(b, l // tm),
        in_specs=[row(Q_LORA), row(KV_LORA), row(QK_ROPE), row(128), row(128)] + [full(a) for a in consts],
        out_specs=[head(QK_DIM), head(QK_DIM), head(V_HEAD)],
        out_shape=[jax.ShapeDtypeStruct((b, N_HEADS, l, QK_DIM), BF16),
                   jax.ShapeDtypeStruct((b, N_HEADS, l, QK_DIM), BF16),
                   jax.ShapeDtypeStruct((b, N_HEADS, l, V_HEAD), BF16)],
        compiler_params=_params(("parallel", "parallel")),
        name="mla_prep",
    )(ql, kvl, kr, cos2, sin2, *consts)


def _attn_kernel(qi_ref, ki_ref, q_ref, k_ref, v_ref, o_ref, m_ref, l_ref, acc_ref):
    p = pl.program_id(2)
    qi = qi_ref[p]
    ki = ki_ref[p]

    @pl.when(ki == 0)
    def _():
        m_ref[...] = jnp.full(m_ref.shape, -jnp.inf, F32)
        l_ref[...] = jnp.zeros(l_ref.shape, F32)
        acc_ref[...] = jnp.zeros(acc_ref.shape, F32)

    def update(masked):
        s = lax.dot_general(q_ref[...], k_ref[...], (((1,), (1,)), ((), ())),
                            preferred_element_type=F32)
        if masked:
            row_chunk = lax.broadcasted_iota(I32, s.shape, 0) // CHUNK
            col_chunk = lax.broadcasted_iota(I32, s.shape, 1) // CHUNK
            s = jnp.where(col_chunk <= row_chunk, s, -jnp.inf)
        m_prev = m_ref[...]
        m_new = jnp.maximum(m_prev, jnp.max(s, axis=-1, keepdims=True))
        alpha = jnp.exp(m_prev - m_new)
        pr = jnp.exp(s - m_new)
        l_ref[...] = alpha * l_ref[...] + jnp.sum(pr, axis=-1, keepdims=True)
        acc_ref[...] = alpha * acc_ref[...] + jnp.dot(pr.astype(BF16), v_ref[...],
                                                      preferred_element_type=F32)
        m_ref[...] = m_new

    @pl.when(ki < qi)
    def _():
        update(False)

    @pl.when(ki == qi)
    def _():
        update(True)
        o_ref[...] = (acc_ref[...] / l_ref[...]).astype(o_ref.dtype)


def _attention(q, k, v):
    b, nh, l, _ = q.shape
    blk = min(ATTN_BLOCK, l)
    nb = l // blk
    pairs = [(i, j) for i in range(nb) for j in range(i + 1)]
    qi_tab = jnp.array([p[0] for p in pairs], I32)
    ki_tab = jnp.array([p[1] for p in pairs], I32)
    grid_spec = pltpu.PrefetchScalarGridSpec(
        num_scalar_prefetch=2,
        grid=(b, nh, len(pairs)),
        in_specs=[
            pl.BlockSpec((None, None, blk, QK_DIM), lambda i, h, p, qt, kt: (i, h, qt[p], 0)),
            pl.BlockSpec((None, None, blk, QK_DIM), lambda i, h, p, qt, kt: (i, h, kt[p], 0)),
            pl.BlockSpec((None, None, blk, V_HEAD), lambda i, h, p, qt, kt: (i, h, kt[p], 0)),
        ],
        out_specs=pl.BlockSpec((None, blk, V_HEAD), lambda i, h, p, qt, kt: (i, qt[p], h)),
        scratch_shapes=[pltpu.VMEM((blk, 1), F32), pltpu.VMEM((blk, 1), F32),
                        pltpu.VMEM((blk, V_HEAD), F32)],
    )
    return pl.pallas_call(
        _attn_kernel,
        grid_spec=grid_spec,
        out_shape=jax.ShapeDtypeStruct((b, l, nh * V_HEAD), BF16),
        compiler_params=_params(("parallel", "parallel", "arbitrary")),
        name="attn",
    )(qi_tab, ki_tab, q, k, v)


def _gelu_tanh(x):
    return x * (0.5 * (1.0 + jnp.tanh(math.sqrt(2.0 / math.pi) * (x + 0.044715 * (x * x * x)))))


def _postmix_kernel(ys_ref, gs_ref, ym_ref, x_ref, mod_ref, sg_ref, mg_ref, wo_ref, fg_ref, wr_ref,
                    br_ref, x1_ref, h2_ref, lg_ref):
    g1 = mod_ref[0, 2:3, :]
    sh2 = mod_ref[0, 3:4, :]
    sc2 = mod_ref[0, 4:5, :]
    ys = _gelu_tanh(ys_ref[0].astype(F32)) * jax.nn.sigmoid(gs_ref[0].astype(F32))
    ys = _rms(ys, sg_ref[...], D_SSM)
    ym = _rms(ym_ref[0].astype(F32), mg_ref[...], D_MLA)
    mix = jnp.concatenate([ys.astype(BF16), ym.astype(BF16)], axis=1)
    x1 = x_ref[0] + g1 * jnp.dot(mix, wo_ref[...], preferred_element_type=F32)
    x1_ref[0] = x1
    h2 = _rms(x1, fg_ref[...], D_MODEL) * (1.0 + sc2) + sh2
    h2_ref[0] = h2
    lg_ref[0] = jnp.dot(h2, wr_ref[...], preferred_element_type=F32) + br_ref[...]


def _postmix(ys, gs, ym, x, mod, sg, mg, wo, fg, wr, br):
    b, l, d = x.shape
    tm = min(ROW_TILE, l)
    row = lambda w: pl.BlockSpec((1, tm, w), lambda i, j: (i, j, 0))
    full = lambda a: pl.BlockSpec(a.shape, lambda i, j: (0,) * a.ndim)
    return pl.pallas_call(
        _postmix_kernel,
        grid=(b, l // tm),
        in_specs=[row(D_SSM), row(D_SSM), row(D_MLA), row(d),
                  pl.BlockSpec((1, N_MOD, d), lambda i, j: (i, 0, 0)),
                  full(sg), full(mg), full(wo), full(fg), full(wr), full(br)],
        out_specs=[row(d), row(d), row(N_EXPERTS)],
        out_shape=[jax.ShapeDtypeStruct((b, l, d), F32), jax.ShapeDtypeStruct((b, l, d), F32),
                   jax.ShapeDtypeStruct((b, l, N_EXPERTS), F32)],
        compiler_params=_params(("parallel", "parallel")),
        name="postmix",
    )(ys, gs, ym, x, mod, sg, mg, wo, fg, wr, br)


def _route_kernel(lg_ref, idx_ref, gate_ref, rank_ref, cnt_ref, carry_ref):
    @pl.when(pl.program_id(0) == 0)
    def _():
        carry_ref[...] = jnp.zeros(carry_ref.shape, F32)

    lg = lg_ref[...]
    tm = lg.shape[0]
    lane = lax.broadcasted_iota(I32, lg.shape, 1).astype(F32)
    k_lane = lax.broadcasted_iota(I32, (tm, TOP_K), 1)
    vals, hots = [], []
    idx_out = jnp.zeros((tm, TOP_K), I32)
    work = lg
    for k in range(TOP_K):
        m = jnp.max(work, axis=-1, keepdims=True)
        sel = jnp.min(jnp.where(work == m, lane, float(N_EXPERTS)), axis=-1, keepdims=True)
        hot = lane == sel
        work = jnp.where(hot, -jnp.inf, work)
        vals.append(m)
        hots.append(hot)
        idx_out = jnp.where(k_lane == k, sel.astype(I32), idx_out)
    ex = [jnp.exp(v - vals[0]) for v in vals]
    den = ex[0] + ex[1] + ex[2] + ex[3]
    gate_out = jnp.zeros((tm, TOP_K), F32)
    for k in range(TOP_K):
        gate_out = jnp.where(k_lane == k, ex[k] / den, gate_out)

    cnt = jnp.zeros(lg.shape, F32)
    for hot in hots:
        cnt = cnt + hot.astype(F32)
    r_i = lax.broadcasted_iota(I32, (tm, tm), 0)
    c_i = lax.broadcasted_iota(I32, (tm, tm), 1)
    tri = (c_i < r_i).astype(BF16)
    base = jnp.dot(tri, cnt.astype(BF16), preferred_element_type=F32) + carry_ref[...]
    rank_out = jnp.zeros((tm, TOP_K), I32)
    for k in range(TOP_K):
        rk = jnp.sum(jnp.where(hots[k], base, 0.0), axis=-1, keepdims=True)
        rank_out = jnp.where(k_lane == k, rk.astype(I32), rank_out)
    carry = carry_ref[...] + jnp.sum(cnt, axis=0, keepdims=True)
    carry_ref[...] = carry
    idx_ref[...] = idx_out
    gate_ref[...] = gate_out
    rank_ref[...] = rank_out
    cnt_ref[...] = carry


def _route(logits):
    t, e = logits.shape
    tm = min(ROUTE_TILE, t)
    small = pl.BlockSpec((tm, TOP_K), lambda i: (i, 0))
    return pl.pallas_call(
        _route_kernel,
        grid=(t // tm,),
        in_specs=[pl.BlockSpec((tm, e), lambda i: (i, 0))],
        out_specs=[small, small, small, pl.BlockSpec((1, e), lambda i: (0, 0))],
        out_shape=[jax.ShapeDtypeStruct((t, TOP_K), I32), jax.ShapeDtypeStruct((t, TOP_K), F32),
                   jax.ShapeDtypeStruct((t, TOP_K), I32), jax.ShapeDtypeStruct((1, e), F32)],
        scratch_shapes=[pltpu.VMEM((1, e), F32)],
        compiler_params=_params(("arbitrary",)),
        name="route",
    )(logits)


def _dispatch_kernel(dest_ref, zpos_ref, h_ref, xs_ref, zero_ref, sem_ref, zsem_ref):
    i = pl.program_id(0)
    td = h_ref.shape[0]

    @pl.when(i == 0)
    def _():
        zero_ref[...] = jnp.zeros(zero_ref.shape, F32)
        n_blocks = xs_ref.shape[0] // MOE_BLOCK
        n_used = zpos_ref[N_EXPERTS]
        def zcopy(start):
            return pltpu.make_async_copy(
                zero_ref, xs_ref.at[pl.ds(pl.multiple_of(start, MOE_BLOCK), MOE_BLOCK)], zsem_ref)
        def zstart(e, c):
            @pl.when(zpos_ref[e] >= 0)
            def _():
                zcopy(zpos_ref[e]).start()
            return c
        lax.fori_loop(0, N_EXPERTS, zstart, 0)
        def tstart(blk, c):
            zcopy(blk * MOE_BLOCK).start()
            return c
        lax.fori_loop(n_used, n_blocks, tstart, 0)
        def zwait(e, c):
            zcopy(0).wait()
            return c
        lax.fori_loop(0, zpos_ref[N_EXPERTS + 1] + n_blocks - n_used, zwait, 0)

    def issue(r, c):
        for k in range(TOP_K):
            d = dest_ref[(i * td + r) * TOP_K + k]
            pltpu.make_async_copy(h_ref.at[pl.ds(r, 1)], xs_ref.at[pl.ds(d, 1)], sem_ref).start()
        return c
    lax.fori_loop(0, td, issue, 0)
    for k in range(TOP_K):
        pltpu.make_async_copy(h_ref, xs_ref.at[pl.ds(0, td)], sem_ref).wait()


def _dispatch(dest, zpos, h2, n_rows):
    t, d = h2.shape
    td = min(DISPATCH_TILE, t)
    grid_spec = pltpu.PrefetchScalarGridSpec(
        num_scalar_prefetch=2,
        grid=(t // td,),
        in_specs=[pl.BlockSpec((td, d), lambda i, dr, zr: (i, 0))],
        out_specs=pl.BlockSpec(memory_space=pl.ANY),
        scratch_shapes=[pltpu.VMEM((MOE_BLOCK, d), F32), pltpu.SemaphoreType.DMA,
                        pltpu.SemaphoreType.DMA],
    )
    return pl.pallas_call(
        _dispatch_kernel,
        grid_spec=grid_spec,
        out_shape=jax.ShapeDtypeStruct((n_rows, d), F32),
        compiler_params=_params(("arbitrary",)),
        name="dispatch",
    )(dest, zpos, h2)


def _experts_kernel(be_ref, nu_ref, xs_ref, bgu_ref, bd_ref, wgu_hbm, wd_hbm, ys_ref, wgu_ref, wd_ref,
                    sem_ref):
    i = pl.program_id(0)
    e = be_ref[i]
    used = i < nu_ref[0]
    prev = be_ref[jnp.maximum(i - 1, 0)]

    @pl.when(jnp.logical_and(used, jnp.logical_or(i == 0, e != prev)))
    def _():
        c1 = pltpu.make_async_copy(wgu_hbm.at[e], wgu_ref, sem_ref.at[0])
        c2 = pltpu.make_async_copy(wd_hbm.at[e], wd_ref, sem_ref.at[1])
        c1.start()
        c2.start()
        c1.wait()
        c2.wait()

    @pl.when(used)
    def _():
        x = xs_ref[...].astype(BF16)
        acc = jnp.zeros(ys_ref.shape, F32) + bd_ref[0]
        for f in range(D_FF // FF_TILE):
            lo, hi = f * FF_TILE, (f + 1) * FF_TILE
            gate = jnp.dot(x, wgu_ref[:, lo:hi], preferred_element_type=F32) + bgu_ref[0, :, lo:hi]
            up = (jnp.dot(x, wgu_ref[:, D_FF + lo:D_FF + hi], preferred_element_type=F32)
                  + bgu_ref[0, :, D_FF + lo:D_FF + hi])
            gate = jnp.minimum(gate, SWIGLU_LIMIT)
            up = jnp.clip(up, -SWIGLU_LIMIT, SWIGLU_LIMIT)
            act = (up + 1.0) * (gate * jax.nn.sigmoid(SWIGLU_ALPHA * gate))
            acc = acc + jnp.dot(act.astype(BF16), wd_ref[lo:hi, :], preferred_element_type=F32)
        ys_ref[...] = acc

    @pl.when(jnp.logical_not(used))
    def _():
        ys_ref[...] = jnp.zeros(ys_ref.shape, F32)


def _experts(block_e, n_used, xs, w_gu, b_gu, w_down, b_down):
    n_rows, d = xs.shape
    m = MOE_BLOCK
    n_blocks = n_rows // m
    last = lambda i, be, nu: jnp.minimum(i, nu[0] - 1)
    grid_spec = pltpu.PrefetchScalarGridSpec(
        num_scalar_prefetch=2,
        grid=(n_blocks,),
        in_specs=[
            pl.BlockSpec((m, d), lambda i, be, nu: (last(i, be, nu), 0)),
            pl.BlockSpec((1, 1, 2 * D_FF), lambda i, be, nu: (be[i], 0, 0)),
            pl.BlockSpec((1, 1, d), lambda i, be, nu: (be[i], 0, 0)),
            pl.BlockSpec(memory_space=pl.ANY),
            pl.BlockSpec(memory_space=pl.ANY),
        ],
        out_specs=pl.BlockSpec((m, d), lambda i, be, nu: (i, 0)),
        scratch_shapes=[pltpu.VMEM((d, 2 * D_FF), BF16), pltpu.VMEM((D_FF, d), BF16),
                        pltpu.SemaphoreType.DMA((2,))],
    )
    return pl.pallas_call(
        _experts_kernel,
        grid_spec=grid_spec,
        out_shape=jax.ShapeDtypeStruct((n_rows, d), F32),
        compiler_params=_params(("arbitrary",)),
        name="experts",
    )(block_e, n_used, xs, b_gu, b_down, w_gu, w_down)


def _combine_kernel(dest_ref, x1_ref, gate_ref, mod_ref, ys_hbm, o_ref, buf_ref, sem_ref):
    i = pl.program_id(0)
    n = pl.num_programs(0)
    tc = x1_ref.shape[0]

    def issue(step, slot):
        def body(r, c):
            for k in range(TOP_K):
                d = dest_ref[(step * tc + r) * TOP_K + k]
                pltpu.make_async_copy(ys_hbm.at[pl.ds(d, 1)], buf_ref.at[slot, k, pl.ds(r, 1)],
                                      sem_ref.at[slot]).start()
            return c
        lax.fori_loop(0, tc, body, 0)

    @pl.when(i == 0)
    def _():
        issue(0, 0)

    slot = i % 2

    @pl.when(i + 1 < n)
    def _():
        issue(i + 1, 1 - slot)

    for k in range(TOP_K):
        pltpu.make_async_copy(ys_hbm.at[pl.ds(0, tc)], buf_ref.at[slot, k], sem_ref.at[slot]).wait()
    g2 = mod_ref[0, 5:6, :]
    gate = gate_ref[...]
    acc = gate[:, 0:1] * buf_ref[slot, 0]
    for k in range(1, TOP_K):
        acc = acc + gate[:, k:k + 1] * buf_ref[slot, k]
    o_ref[...] = x1_ref[...] + g2 * acc


def _combine(dest, x1, gate, mod, ys, seq_len):
    t, d = x1.shape
    tc = min(COMBINE_TILE, seq_len)
    per_seq = seq_len // tc
    grid_spec = pltpu.PrefetchScalarGridSpec(
        num_scalar_prefetch=1,
        grid=(t // tc,),
        in_specs=[
            pl.BlockSpec((tc, d), lambda i, dr: (i, 0)),
            pl.BlockSpec((tc, TOP_K), lambda i, dr: (i, 0)),
            pl.BlockSpec((1, N_MOD, d), lambda i, dr: (i // per_seq, 0, 0)),
            pl.BlockSpec(memory_space=pl.ANY),
        ],
        out_specs=pl.BlockSpec((tc, d), lambda i, dr: (i, 0)),
        scratch_shapes=[pltpu.VMEM((2, TOP_K, tc, d), F32), pltpu.SemaphoreType.DMA((2,))],
    )
    return pl.pallas_call(
        _combine_kernel,
        grid_spec=grid_spec,
        out_shape=jax.ShapeDtypeStruct((t, d), F32),
        compiler_params=_params(("arbitrary",)),
        name="combine",
    )(dest, x1, gate, mod, ys)


def _moe(h2, logits, x1, mod, w_gu, b_gu, w_down, b_down, seq_len):
    t, d = h2.shape
    m, e = MOE_BLOCK, N_EXPERTS
    idx, gate, rank, cnt = _route(logits)
    counts = cnt[0].astype(I32)
    padded = ((counts + m - 1) // m) * m
    pad_ends = jnp.cumsum(padded)
    pad_starts = pad_ends - padded
    dest = (pad_starts[idx] + rank).reshape(-1).astype(I32)
    n_blocks = -(-(t * TOP_K) // m) + e
    n_rows = n_blocks * m
    block_e = jnp.minimum(jnp.searchsorted(pad_ends, jnp.arange(n_blocks) * m, side="right"),
                          e - 1).astype(I32)
    n_used = (pad_ends[-1:] // m).astype(I32)
    zpos = jnp.concatenate([jnp.where(padded > 0, pad_ends - m, -1), n_used,
                            jnp.sum(padded > 0, keepdims=True)]).astype(I32)
    xs = _dispatch(dest, zpos, h2, n_rows)
    ys = _experts(block_e, n_used, xs, w_gu, b_gu, w_down, b_down)
    return _combine(dest, x1, gate, mod, ys, seq_len)


def kernel(x, c, positions, w_ada, b_ada, norm_mix_g, w_in, ssm_A_re, ssm_A_im, ssm_B_re, ssm_B_im, ssm_C_re, ssm_C_im, ssm_D, ssm_log_dt, q_lat_g, kv_lat_g, w_uq, w_ukv, q_nope_g, q_rope_g, k_nope_g, k_rope_g, out_ssm_g, out_mla_g, w_out, norm_ffn_g, w_router, b_router, w_gate_up, b_gate_up, w_down, b_down):
    b, l, d = x.shape
    depth = w_ada.shape[0]
    row = lambda v: v.reshape(1, -1).astype(F32)

    inv_freq = ROPE_BASE ** (-jnp.arange(0, QK_ROPE, 2, dtype=F32) / QK_ROPE)
    ang = positions.astype(F32)[..., None] * inv_freq
    cos, sin = jnp.cos(ang), jnp.sin(ang)
    cos2 = jnp.concatenate([cos, cos, cos, cos], axis=-1)
    sin2 = jnp.concatenate([-sin, sin, -sin, sin], axis=-1)
    c_pad = jnp.zeros((8, d), F32).at[:b].set(c)
    lane_head = jnp.arange(N_HEADS * QK_ROPE) // QK_ROPE
    seg = (lane_head[:, None] == lane_head[None, :]).astype(BF16)

    for li in range(depth):
        mod = _ada(c_pad, w_ada[li], b_ada[li].reshape(1, -1))[:b].reshape(b, N_MOD, d)
        u, gs, ql, kvl, kr = _inproj(x, mod, row(norm_mix_g[li]), w_in[li].astype(BF16))

        tables = _ssm_tables(ssm_A_re[li], ssm_A_im[li], ssm_B_re[li], ssm_B_im[li], ssm_C_re[li],
                             ssm_C_im[li], ssm_D[li], ssm_log_dt[li])
        y_s5 = _s5(u, tables)

        wq = w_uq[li].reshape(Q_LORA, N_HEADS, QK_DIM)
        wq = jnp.concatenate([wq[:, :, :QK_NOPE].reshape(Q_LORA, -1),
                              wq[:, :, QK_NOPE:].reshape(Q_LORA, -1)], axis=1).astype(BF16)
        wkv = w_ukv[li].reshape(KV_LORA, N_HEADS, QK_NOPE + V_HEAD)
        wkv = jnp.concatenate([wkv[:, :, :QK_NOPE].reshape(KV_LORA, -1),
                               wkv[:, :, QK_NOPE:].reshape(KV_LORA, -1)], axis=1).astype(BF16)
        q, k, v = _mla_prep(ql, kvl, kr, cos2, sin2, row(q_lat_g[li]), row(kv_lat_g[li]),
                            row(q_nope_g[li]), row(jnp.tile(q_rope_g[li], N_HEADS)), row(k_nope_g[li]),
                            row(k_rope_g[li]), wq, wkv, seg)
        y_mla = _attention(q, k, v)

        x1, h2, logits = _postmix(y_s5, gs, y_mla, x, mod, row(out_ssm_g[li]), row(out_mla_g[li]),
                                  w_out[li].astype(BF16), row(norm_ffn_g[li]), w_router[li].astype(F32),
                                  row(b_router[li]))

        w_gu = jnp.concatenate([w_gate_up[li][:, :, 0::2], w_gate_up[li][:, :, 1::2]], axis=-1).astype(BF16)
        b_gu = jnp.concatenate([b_gate_up[li][:, 0::2], b_gate_up[li][:, 1::2]], axis=-1)
        out = _moe(h2.reshape(b * l, d), logits.reshape(b * l, N_EXPERTS), x1.reshape(b * l, d), mod,
                   w_gu, b_gu.reshape(N_EXPERTS, 1, 2 * D_FF).astype(F32), w_down[li].astype(BF16),
                   b_down[li].reshape(N_EXPERTS, 1, d).astype(F32), l)
        x = out.reshape(b, l, d)
    return x
```

```python
import functools
import math

import jax
import jax.numpy as jnp
from jax import lax
from jax.experimental import pallas as pl
from jax.experimental.pallas import tpu as pltpu

F32 = jnp.float32
BF16 = jnp.bfloat16
I32 = jnp.int32

D_MODEL = 2048
CHUNK = 64
EPS = 1e-6
D_SSM = 1024
SSM_H = 16
SSM_G = D_SSM // SSM_H
SSM_P = 64
SSM_LC = 64
SSM_SB = 16
SSM_NSB = SSM_LC // SSM_SB
SSM_W = SSM_LC * SSM_H
SSM_BW = SSM_SB * SSM_H
N_HEADS = 8
QK_NOPE = 128
QK_ROPE = 64
V_HEAD = 128
D_MLA = N_HEADS * V_HEAD
Q_LORA = 512
KV_LORA = 256
QK_DIM = QK_NOPE + QK_ROPE
ROPE_BASE = 10000.0
D_IN = 2 * D_SSM + Q_LORA + KV_LORA + QK_ROPE
N_EXPERTS = 32
TOP_K = 4
D_FF = D_MODEL
SWIGLU_LIMIT = 7.0
SWIGLU_ALPHA = 1.702
MOE_BLOCK = 256
N_MOD = 6

VMEM_LIMIT = 56 * 1024 * 1024

ROW_TILE = 256
ATTN_BLOCK = 1024
ATTN_HEADS = 2
ATTN_KSUB = 1024
MASK_VALUE = -1e30
ROUTE_TILE = 512
DISPATCH_TILE = 256
COMBINE_TILE = 128
FF_TILE = 512
STAGE_ROWS = 1024


def _params(sem, limit=VMEM_LIMIT):
    return pltpu.CompilerParams(dimension_semantics=sem, vmem_limit_bytes=limit)


def _ada_kernel(c_ref, w_ref, b_ref, o_ref):
    c = c_ref[...]
    ca = c * jax.nn.sigmoid(c)
    o_ref[...] = jnp.dot(ca, w_ref[...], preferred_element_type=F32) + b_ref[...]


def _ada(c_pad, w_ada, b_ada):
    n = w_ada.shape[1]
    tn = 1024
    return pl.pallas_call(
        _ada_kernel,
        grid=(n // tn,),
        in_specs=[
            pl.BlockSpec((8, D_MODEL), lambda j: (0, 0)),
            pl.BlockSpec((D_MODEL, tn), lambda j: (0, j)),
            pl.BlockSpec((1, tn), lambda j: (0, j)),
        ],
        out_specs=pl.BlockSpec((8, tn), lambda j: (0, j)),
        out_shape=jax.ShapeDtypeStruct((8, n), F32),
        compiler_params=_params(("arbitrary",)),
        name="ada",
    )(c_pad, w_ada, b_ada)


def _inproj_kernel(x_ref, mod_ref, g_ref, w_ref, u_ref, gs_ref, ql_ref, kvl_ref, kr_ref):
    x = x_ref[0]
    sh = mod_ref[0, 0:1, :]
    sc = mod_ref[0, 1:2, :]
    h = x * lax.rsqrt(jnp.mean(x * x, axis=-1, keepdims=True) + EPS) * g_ref[...]
    h = h * (1.0 + sc) + sh
    z = jnp.dot(h.astype(BF16), w_ref[...], preferred_element_type=F32)
    s1, s2, s3 = 2 * D_SSM, 2 * D_SSM + Q_LORA, 2 * D_SSM + Q_LORA + KV_LORA
    u_ref[0] = z[:, :D_SSM].astype(BF16)
    gs_ref[0] = z[:, D_SSM:s1].astype(BF16)
    ql_ref[0] = z[:, s1:s2].astype(BF16)
    kvl_ref[0] = z[:, s2:s3].astype(BF16)
    kr_ref[0] = z[:, s3:].astype(BF16)


def _inproj(x, mod, g, w_in):
    b, l, d = x.shape
    tm = min(ROW_TILE, l)
    widths = (D_SSM, D_SSM, Q_LORA, KV_LORA, QK_ROPE)
    return pl.pallas_call(
        _inproj_kernel,
        grid=(b, l // tm),
        in_specs=[
            pl.BlockSpec((1, tm, d), lambda i, j: (i, j, 0)),
            pl.BlockSpec((1, N_MOD, d), lambda i, j: (i, 0, 0)),
            pl.BlockSpec((1, d), lambda i, j: (0, 0)),
            pl.BlockSpec((d, D_IN), lambda i, j: (0, 0)),
        ],
        out_specs=[pl.BlockSpec((1, tm, w), lambda i, j: (i, j, 0)) for w in widths],
        out_shape=[jax.ShapeDtypeStruct((b, l, w), BF16) for w in widths],
        compiler_params=_params(("parallel", "parallel")),
        name="inproj",
    )(x, mod, g, w_in)


def _ssm_tables(a_re, a_im, b_re, b_im, c_re, c_im, d_skip, log_dt):
    g, p, h = SSM_G, SSM_P, SSM_H
    a = lax.complex(a_re.astype(F32), a_im.astype(F32))
    dta = a * jnp.exp(log_dt.astype(F32))[:, None]
    a_bar = jnp.exp(dta)
    b_bar = ((a_bar - 1.0) / a)[..., None] * lax.complex(b_re.astype(F32), b_im.astype(F32))
    cc = lax.complex(c_re.astype(F32), c_im.astype(F32))
    pw = jnp.exp(jnp.arange(SSM_LC + 1, dtype=F32)[:, None, None] * dta[None])
    kern = jnp.einsum("ghp,jgp,gpk->gjhk", cc, pw[:SSM_LC], b_bar).real
    eye = jnp.eye(h, dtype=F32) * d_skip.astype(F32).reshape(g, 1, h)
    kern = kern.at[:, 0].add(eye)
    dd = jnp.arange(SSM_NSB)[:, None, None]
    ss = jnp.arange(SSM_SB)[None, :, None]
    tt = jnp.arange(SSM_SB)[None, None, :]
    lag = SSM_SB * dd + tt - ss
    kg = kern[:, jnp.clip(lag, 0, SSM_LC - 1)]
    kg = jnp.where((lag >= 0)[None, :, :, :, None, None], kg, 0.0)
    dblk = kg.transpose(0, 1, 2, 5, 3, 4).reshape(g, SSM_NSB, SSM_BW, SSM_BW).astype(BF16)

    wb = pw[SSM_LC - 1 - jnp.arange(SSM_LC)][:, :, :, None] * b_bar[None]
    wb = wb.transpose(1, 0, 3, 2).reshape(g, SSM_W, p)
    wb_re = wb.real.reshape(g // 2, 2, SSM_W, p)
    wb_im = wb.imag.reshape(g // 2, 2, SSM_W, p)
    z = jnp.zeros_like(wb_re[:, 0])
    wb_even = jnp.concatenate([wb_re[:, 0], z, wb_im[:, 0], z], axis=-1)
    wb_odd = jnp.concatenate([z, wb_re[:, 1], z, wb_im[:, 1]], axis=-1)
    wb_pair = jnp.stack([wb_even, wb_odd], axis=1).astype(BF16)

    cp = cc[:, None, :, :] * pw[1:, :, None, :].transpose(1, 0, 2, 3)
    wc = cp.transpose(0, 3, 1, 2).reshape(g, p, SSM_W)
    wc_re = wc.real.reshape(g // 2, 2, p, SSM_W)
    wc_im = (-wc.imag).reshape(g // 2, 2, p, SSM_W)
    zc = jnp.zeros_like(wc_re[:, 0])
    wc_pair = jnp.concatenate([
        jnp.concatenate([wc_re[:, 0], zc], axis=-1),
        jnp.concatenate([zc, wc_re[:, 1]], axis=-1),
        jnp.concatenate([wc_im[:, 0], zc], axis=-1),
        jnp.concatenate([zc, wc_im[:, 1]], axis=-1),
    ], axis=1).astype(BF16)

    a_blk = jnp.stack([pw[SSM_LC].real.reshape(-1), pw[SSM_LC].imag.reshape(-1)])
    return dblk, wb_pair, wc_pair, a_blk


def _ssm_in_kernel(u_ref, wb_ref, vre_ref, vim_ref):
    v = (jnp.dot(u_ref[0], wb_ref[0, 0], preferred_element_type=F32)
         + jnp.dot(u_ref[1], wb_ref[0, 1], preferred_element_type=F32))
    vre_ref[...] = v[:, :128]
    vim_ref[...] = v[:, 128:]


def _ssm_in(u_g, wb_pair):
    g, r, _ = u_g.shape
    out = jax.ShapeDtypeStruct((r, g * SSM_P), F32)
    return pl.pallas_call(
        _ssm_in_kernel,
        grid=(g // 2,),
        in_specs=[
            pl.BlockSpec((2, r, SSM_W), lambda j: (j, 0, 0)),
            pl.BlockSpec((1, 2, SSM_W, 256), lambda j: (j, 0, 0, 0)),
        ],
        out_specs=[pl.BlockSpec((r, 128), lambda j: (0, j))] * 2,
        out_shape=[out, out],
        compiler_params=_params(("parallel",)),
        name="ssm_in",
    )(u_g, wb_pair)


def _ssm_scan_kernel(vre_ref, vim_ref, a_ref, xre_ref, xim_ref):
    ar = a_ref[0:1, :]
    ai = a_ref[1:2, :]
    b, nc, lt = vre_ref.shape

    def body(c, carry):
        sre, sim = carry
        xre_ref[:, pl.ds(c, 1), :] = sre
        xim_ref[:, pl.ds(c, 1), :] = sim
        vre = vre_ref[:, pl.ds(c, 1), :]
        vim = vim_ref[:, pl.ds(c, 1), :]
        return ar * sre - ai * sim + vre, ar * sim + ai * sre + vim

    zero = jnp.zeros((b, 1, lt), F32)
    lax.fori_loop(0, nc, body, (zero, zero))


def _ssm_scan(vre, vim, a_blk):
    b, nc, n = vre.shape
    lt = 1024
    spec = pl.BlockSpec((b, nc, lt), lambda j: (0, 0, j))
    out = jax.ShapeDtypeStruct((b, nc, n), F32)
    return pl.pallas_call(
        _ssm_scan_kernel,
        grid=(n // lt,),
        in_specs=[spec, spec, pl.BlockSpec((2, lt), lambda j: (0, j))],
        out_specs=[spec, spec],
        out_shape=[out, out],
        compiler_params=_params(("parallel",)),
        name="ssm_scan",
    )(vre, vim, a_blk)


def _ssm_out_kernel(u_ref, d_ref, xre_ref, xim_ref, wc_ref, y_ref):
    xcat = jnp.concatenate([xre_ref[...], xim_ref[...]], axis=1).astype(BF16)
    yx = jnp.dot(xcat, wc_ref[0], preferred_element_type=F32)
    for gi in range(2):
        for i in range(SSM_NSB):
            acc = yx[:, gi * SSM_W + i * SSM_BW: gi * SSM_W + (i + 1) * SSM_BW]
            for j in range(i + 1):
                acc = acc + jnp.dot(u_ref[gi, :, j * SSM_BW:(j + 1) * SSM_BW], d_ref[gi, i - j],
                                    preferred_element_type=F32)
            y_ref[gi, :, i * SSM_BW:(i + 1) * SSM_BW] = acc.astype(BF16)


def _ssm_out(u_g, dblk, xre, xim, wc_pair):
    g, r, _ = u_g.shape
    return pl.pallas_call(
        _ssm_out_kernel,
        grid=(g // 2,),
        in_specs=[
            pl.BlockSpec((2, r, SSM_W), lambda j: (j, 0, 0)),
            pl.BlockSpec((2, SSM_NSB, SSM_BW, SSM_BW), lambda j: (j, 0, 0, 0)),
            pl.BlockSpec((r, 128), lambda j: (0, j)),
            pl.BlockSpec((r, 128), lambda j: (0, j)),
            pl.BlockSpec((1, 256, 2 * SSM_W), lambda j: (j, 0, 0)),
        ],
        out_specs=pl.BlockSpec((2, r, SSM_W), lambda j: (j, 0, 0)),
        out_shape=jax.ShapeDtypeStruct((g, r, SSM_W), BF16),
        compiler_params=_params(("parallel",)),
        name="ssm_out",
    )(u_g, dblk, xre, xim, wc_pair)


def _s5(u, tables):
    dblk, wb_pair, wc_pair, a_blk = tables
    b, l, _ = u.shape
    nc = l // SSM_LC
    u_g = u.reshape(b, nc, SSM_LC, SSM_G, SSM_H).transpose(3, 0, 1, 2, 4).reshape(SSM_G, b * nc, SSM_W)
    vre, vim = _ssm_in(u_g, wb_pair)
    n = SSM_G * SSM_P
    xre, xim = _ssm_scan(vre.reshape(b, nc, n), vim.reshape(b, nc, n), a_blk)
    y_g = _ssm_out(u_g, dblk, xre.reshape(b * nc, n), xim.reshape(b * nc, n), wc_pair)
    return y_g.reshape(SSM_G, b, nc, SSM_LC, SSM_H).transpose(1, 2, 3, 0, 4).reshape(b, l, D_SSM)


def _rms(x, g, n):
    return x * lax.rsqrt(jnp.sum(x * x, axis=-1, keepdims=True) * (1.0 / n) + EPS) * g


def _mla_prep_kernel(ql_ref, kvl_ref, kr_ref, cos_ref, sin_ref, qlg_ref, kvlg_ref, qng_ref, qrg_ref,
                     kng_ref, krg_ref, wq_ref, wkv_ref, seg_ref, q_ref, k_ref, v_ref):
    scale = QK_DIM ** -0.5 * math.log2(math.e)
    cs = cos_ref[0]
    sn = sin_ref[0]
    ql = _rms(ql_ref[0].astype(F32), qlg_ref[...], Q_LORA)
    q = jnp.dot(ql.astype(BF16), wq_ref[...], preferred_element_type=F32)
    kvl = _rms(kvl_ref[0].astype(F32), kvlg_ref[...], KV_LORA)
    kv = jnp.dot(kvl.astype(BF16), wkv_ref[...], preferred_element_type=F32)

    nr = N_HEADS * QK_ROPE
    qr = q[:, N_HEADS * QK_NOPE:]
    sq = qr * qr
    sq_hi = sq.astype(BF16)
    sq_lo = (sq - sq_hi.astype(F32)).astype(BF16)
    ssq = (jnp.dot(sq_hi, seg_ref[...], preferred_element_type=F32)
           + jnp.dot(sq_lo, seg_ref[...], preferred_element_type=F32))
    qr = qr * lax.rsqrt(ssq * (1.0 / QK_ROPE) + EPS) * qrg_ref[...]
    lane = lax.broadcasted_iota(I32, qr.shape, 1)
    first_half = (lane % QK_ROPE) < (QK_ROPE // 2)
    partner = jnp.where(first_half, pltpu.roll(qr, nr - QK_ROPE // 2, axis=1),
                        pltpu.roll(qr, QK_ROPE // 2, axis=1))
    cs4 = jnp.concatenate([cs] * (nr // 128), axis=1)
    sn4 = jnp.concatenate([sn] * (nr // 128), axis=1)
    qr = (qr * cs4 + partner * sn4) * scale

    kr = _rms(kr_ref[0].astype(F32), krg_ref[...], QK_ROPE)
    kr_partner = jnp.concatenate([kr[:, QK_ROPE // 2:], kr[:, :QK_ROPE // 2]], axis=1)
    kr = (kr * cs[:, :QK_ROPE] + kr_partner * sn[:, :QK_ROPE]).astype(BF16)

    for h in range(N_HEADS):
        qn = _rms(q[:, h * QK_NOPE:(h + 1) * QK_NOPE], qng_ref[...], QK_NOPE) * scale
        q_ref[0, h, :, :QK_NOPE] = qn.astype(BF16)
        q_ref[0, h, :, QK_NOPE:] = qr[:, h * QK_ROPE:(h + 1) * QK_ROPE].astype(BF16)
        kn = _rms(kv[:, h * QK_NOPE:(h + 1) * QK_NOPE], kng_ref[...], QK_NOPE)
        k_ref[0, h, :, :QK_NOPE] = kn.astype(BF16)
        k_ref[0, h, :, QK_NOPE:] = kr
        v_ref[0, h, :, :V_HEAD] = kv[:, N_HEADS * QK_NOPE + h * V_HEAD:
                                     N_HEADS * QK_NOPE + (h + 1) * V_HEAD].astype(BF16)
        v_ref[0, h, :, V_HEAD:] = jnp.ones((v_ref.shape[2], V_HEAD), BF16)


def _mla_prep(ql, kvl, kr, cos2, sin2, qlg, kvlg, qng, qrg, kng, krg, wq, wkv, seg):
    b, l, _ = ql.shape
    tm = min(ROW_TILE, l)
    row = lambda w: pl.BlockSpec((1, tm, w), lambda i, j: (i, j, 0))
    full = lambda a: pl.BlockSpec(a.shape, lambda i, j: (0,) * a.ndim)
    head = lambda w: pl.BlockSpec((1, N_HEADS, tm, w), lambda i, j: (i, 0, j, 0))
    consts = (qlg, kvlg, qng, qrg, kng, krg, wq, wkv, seg)
    return pl.pallas_call(
        _mla_prep_kernel,
        grid=(b, l // tm),
        in_specs=[row(Q_LORA), row(KV_LORA), row(QK_ROPE), row(128), row(128)] + [full(a) for a in consts],
        out_specs=[head(QK_DIM), head(QK_DIM), head(2 * V_HEAD)],
        out_shape=[jax.ShapeDtypeStruct((b, N_HEADS, l, QK_DIM), BF16),
                   jax.ShapeDtypeStruct((b, N_HEADS, l, QK_DIM), BF16),
                   jax.ShapeDtypeStruct((b, N_HEADS, l, 2 * V_HEAD), BF16)],
        compiler_params=_params(("parallel", "parallel")),
        name="mla_prep",
    )(ql, kvl, kr, cos2, sin2, *consts)


def _attn_kernel(qi_ref, ki_ref, q_ref, k_ref, v_ref, o_ref, m_ref, acc_ref):
    p = pl.program_id(2)
    qi = qi_ref[p]
    ki = ki_ref[p]

    @pl.when(ki == 0)
    def _():
        m_ref[...] = jnp.full(m_ref.shape, MASK_VALUE, F32)
        acc_ref[...] = jnp.zeros(acc_ref.shape, F32)

    def update(masked):
        blk = q_ref.shape[1]
        ks = min(ATTN_KSUB, blk)
        if masked:
            row_chunk = lax.broadcasted_iota(I32, (blk, ks), 0) // CHUNK
            col_chunk = lax.broadcasted_iota(I32, (blk, ks), 1) // CHUNK
        for h in range(ATTN_HEADS):
            q = q_ref[h]
            m_parts, o_parts = [], []
            for j in range(blk // ks):
                s = lax.dot_general(q, k_ref[h, j * ks:(j + 1) * ks, :], (((1,), (1,)), ((), ())),
                                    preferred_element_type=F32)
                if masked:
                    s = jnp.where(col_chunk + (j * ks) // CHUNK <= row_chunk, s, MASK_VALUE)
                m_j = jnp.max(s, axis=-1, keepdims=True)
                pr = jnp.exp2((s - m_j).astype(BF16))
                o_parts.append(jnp.dot(pr, v_ref[h, j * ks:(j + 1) * ks, :], preferred_element_type=F32))
                m_parts.append(m_j)
            m_prev = m_ref[h]
            m_new = m_prev
            for m_j in m_parts:
                m_new = jnp.maximum(m_new, m_j)
            acc = jnp.exp2(m_prev - m_new) * acc_ref[h]
            for m_j, o_j in zip(m_parts, o_parts):
                acc = acc + jnp.exp2(m_j - m_new) * o_j
            acc_ref[h] = acc
            m_ref[h] = m_new

    @pl.when(ki < qi)
    def _():
        update(False)

    @pl.when(ki == qi)
    def _():
        update(True)
        for h in range(ATTN_HEADS):
            acc = acc_ref[h]
            o_ref[:, h * V_HEAD:(h + 1) * V_HEAD] = (
                acc[:, :V_HEAD] / acc[:, V_HEAD:V_HEAD + 1]).astype(o_ref.dtype)


def _attention(q, k, v):
    b, nh, l, _ = q.shape
    blk = min(ATTN_BLOCK, l)
    nb = l // blk
    hb = ATTN_HEADS
    pairs = [(i, j) for i in range(nb) for j in range(i + 1)]
    qi_tab = jnp.array([p[0] for p in pairs], I32)
    ki_tab = jnp.array([p[1] for p in pairs], I32)
    grid_spec = pltpu.PrefetchScalarGridSpec(
        num_scalar_prefetch=2,
        grid=(b, nh // hb, len(pairs)),
        in_specs=[
            pl.BlockSpec((None, hb, blk, QK_DIM), lambda i, h, p, qt, kt: (i, h, qt[p], 0)),
            pl.BlockSpec((None, hb, blk, QK_DIM), lambda i, h, p, qt, kt: (i, h, kt[p], 0)),
            pl.BlockSpec((None, hb, blk, 2 * V_HEAD), lambda i, h, p, qt, kt: (i, h, kt[p], 0)),
        ],
        out_specs=pl.BlockSpec((None, blk, hb * V_HEAD), lambda i, h, p, qt, kt: (i, qt[p], h)),
        scratch_shapes=[pltpu.VMEM((hb, blk, 1), F32), pltpu.VMEM((hb, blk, 2 * V_HEAD), F32)],
    )
    return pl.pallas_call(
        _attn_kernel,
        grid_spec=grid_spec,
        out_shape=jax.ShapeDtypeStruct((b, l, nh * V_HEAD), BF16),
        compiler_params=_params(("parallel", "parallel", "arbitrary")),
        name="attn",
    )(qi_tab, ki_tab, q, k, v)


def _gelu_tanh(x):
    return x * (0.5 * (1.0 + jnp.tanh(math.sqrt(2.0 / math.pi) * (x + 0.044715 * (x * x * x)))))


def _postmix_kernel(ys_ref, gs_ref, ym_ref, x_ref, mod_ref, sg_ref, mg_ref, wo_ref, fg_ref, wr_ref,
                    br_ref, x1_ref, h2_ref, lg_ref):
    g1 = mod_ref[0, 2:3, :]
    sh2 = mod_ref[0, 3:4, :]
    sc2 = mod_ref[0, 4:5, :]
    ys = _gelu_tanh(ys_ref[0].astype(F32)) * jax.nn.sigmoid(gs_ref[0].astype(F32))
    ys = _rms(ys, sg_ref[...], D_SSM)
    ym = _rms(ym_ref[0].astype(F32), mg_ref[...], D_MLA)
    mix = jnp.concatenate([ys.astype(BF16), ym.astype(BF16)], axis=1)
    x1 = x_ref[0] + g1 * jnp.dot(mix, wo_ref[...], preferred_element_type=F32)
    x1_ref[0] = x1
    h2 = _rms(x1, fg_ref[...], D_MODEL) * (1.0 + sc2) + sh2
    h2_ref[0] = h2
    lg_ref[0] = jnp.dot(h2, wr_ref[...], preferred_element_type=F32) + br_ref[...]


def _postmix(ys, gs, ym, x, mod, sg, mg, wo, fg, wr, br):
    b, l, d = x.shape
    tm = min(ROW_TILE, l)
    row = lambda w: pl.BlockSpec((1, tm, w), lambda i, j: (i, j, 0))
    full = lambda a: pl.BlockSpec(a.shape, lambda i, j: (0,) * a.ndim)
    return pl.pallas_call(
        _postmix_kernel,
        grid=(b, l // tm),
        in_specs=[row(D_SSM), row(D_SSM), row(D_MLA), row(d),
                  pl.BlockSpec((1, N_MOD, d), lambda i, j: (i, 0, 0)),
                  full(sg), full(mg), full(wo), full(fg), full(wr), full(br)],
        out_specs=[row(d), row(d), row(N_EXPERTS)],
        out_shape=[jax.ShapeDtypeStruct((b, l, d), F32), jax.ShapeDtypeStruct((b, l, d), F32),
                   jax.ShapeDtypeStruct((b, l, N_EXPERTS), F32)],
        compiler_params=_params(("parallel", "parallel")),
        name="postmix",
    )(ys, gs, ym, x, mod, sg, mg, wo, fg, wr, br)


def _route_kernel(lg_ref, idx_ref, gate_ref, rank_ref, cnt_ref, carry_ref):
    @pl.when(pl.program_id(0) == 0)
    def _():
        carry_ref[...] = jnp.zeros(carry_ref.shape, F32)

    lg = lg_ref[...]
    tm = lg.shape[0]
    lane = lax.broadcasted_iota(I32, lg.shape, 1).astype(F32)
    k_lane = lax.broadcasted_iota(I32, (tm, TOP_K), 1)
    vals, hots = [], []
    idx_out = jnp.zeros((tm, TOP_K), I32)
    work = lg
    for k in range(TOP_K):
        m = jnp.max(work, axis=-1, keepdims=True)
        sel = jnp.min(jnp.where(work == m, lane, float(N_EXPERTS)), axis=-1, keepdims=True)
        hot = lane == sel
        work = jnp.where(hot, -jnp.inf, work)
        vals.append(m)
        hots.append(hot)
        idx_out = jnp.where(k_lane == k, sel.astype(I32), idx_out)
    ex = [jnp.exp(v - vals[0]) for v in vals]
    den = ex[0] + ex[1] + ex[2] + ex[3]
    gate_out = jnp.zeros((tm, TOP_K), F32)
    for k in range(TOP_K):
        gate_out = jnp.where(k_lane == k, ex[k] / den, gate_out)

    cnt = jnp.zeros(lg.shape, F32)
    for hot in hots:
        cnt = cnt + hot.astype(F32)
    r_i = lax.broadcasted_iota(I32, (tm, tm), 0)
    c_i = lax.broadcasted_iota(I32, (tm, tm), 1)
    tri = (c_i < r_i).astype(BF16)
    base = jnp.dot(tri, cnt.astype(BF16), preferred_element_type=F32) + carry_ref[...]
    rank_out = jnp.zeros((tm, TOP_K), I32)
    for k in range(TOP_K):
        rk = jnp.sum(jnp.where(hots[k], base, 0.0), axis=-1, keepdims=True)
        rank_out = jnp.where(k_lane == k, rk.astype(I32), rank_out)
    carry = carry_ref[...] + jnp.sum(cnt, axis=0, keepdims=True)
    carry_ref[...] = carry
    idx_ref[...] = idx_out
    gate_ref[...] = gate_out
    rank_ref[...] = rank_out
    cnt_ref[...] = carry


def _route(logits):
    t, e = logits.shape
    tm = min(ROUTE_TILE, t)
    small = pl.BlockSpec((tm, TOP_K), lambda i: (i, 0))
    return pl.pallas_call(
        _route_kernel,
        grid=(t // tm,),
        in_specs=[pl.BlockSpec((tm, e), lambda i: (i, 0))],
        out_specs=[small, small, small, pl.BlockSpec((1, e), lambda i: (0, 0))],
        out_shape=[jax.ShapeDtypeStruct((t, TOP_K), I32), jax.ShapeDtypeStruct((t, TOP_K), F32),
                   jax.ShapeDtypeStruct((t, TOP_K), I32), jax.ShapeDtypeStruct((1, e), F32)],
        scratch_shapes=[pltpu.VMEM((1, e), F32)],
        compiler_params=_params(("arbitrary",)),
        name="route",
    )(logits)


def _dispatch_kernel(dest_ref, zpos_ref, h_ref, xs_ref, zero_ref, sem_ref, zsem_ref):
    i = pl.program_id(0)
    td = h_ref.shape[0]

    @pl.when(i == 0)
    def _():
        zero_ref[...] = jnp.zeros(zero_ref.shape, F32)
        n_blocks = xs_ref.shape[0] // MOE_BLOCK
        n_used = zpos_ref[N_EXPERTS]
        def zcopy(start):
            return pltpu.make_async_copy(
                zero_ref, xs_ref.at[pl.ds(pl.multiple_of(start, MOE_BLOCK), MOE_BLOCK)], zsem_ref)
        def zstart(e, c):
            @pl.when(zpos_ref[e] >= 0)
            def _():
                zcopy(zpos_ref[e]).start()
            return c
        lax.fori_loop(0, N_EXPERTS, zstart, 0)
        def tstart(blk, c):
            zcopy(blk * MOE_BLOCK).start()
            return c
        lax.fori_loop(n_used, n_blocks, tstart, 0)
        def zwait(e, c):
            zcopy(0).wait()
            return c
        lax.fori_loop(0, zpos_ref[N_EXPERTS + 1] + n_blocks - n_used, zwait, 0)

    def issue(r, c):
        for k in range(TOP_K):
            d = dest_ref[(i * td + r) * TOP_K + k]
            pltpu.make_async_copy(h_ref.at[pl.ds(r, 1)], xs_ref.at[pl.ds(d, 1)], sem_ref).start()
        return c
    lax.fori_loop(0, td, issue, 0)
    for k in range(TOP_K):
        pltpu.make_async_copy(h_ref, xs_ref.at[pl.ds(0, td)], sem_ref).wait()


def _dispatch(dest, zpos, h2, n_rows):
    t, d = h2.shape
    td = min(DISPATCH_TILE, t)
    grid_spec = pltpu.PrefetchScalarGridSpec(
        num_scalar_prefetch=2,
        grid=(t // td,),
        in_specs=[pl.BlockSpec((td, d), lambda i, dr, zr: (i, 0))],
        out_specs=pl.BlockSpec(memory_space=pl.ANY),
        scratch_shapes=[pltpu.VMEM((MOE_BLOCK, d), F32), pltpu.SemaphoreType.DMA,
                        pltpu.SemaphoreType.DMA],
    )
    return pl.pallas_call(
        _dispatch_kernel,
        grid_spec=grid_spec,
        out_shape=jax.ShapeDtypeStruct((n_rows, d), F32),
        compiler_params=_params(("arbitrary",)),
        name="dispatch",
    )(dest, zpos, h2)


def _experts_kernel(be_ref, nu_ref, xs_ref, bgu_ref, bd_ref, wt_hbm, wd_hbm, ys_ref, stage_ref, wg_ref,
                    wu_ref, wd_ref, sem_ref):
    i = pl.program_id(0)
    e = be_ref[i]
    used = i < nu_ref[0]
    prev = be_ref[jnp.maximum(i - 1, 0)]
    nt = (((1,), (1,)), ((), ()))

    @pl.when(jnp.logical_and(used, jnp.logical_or(i == 0, e != prev)))
    def _():
        n_stage = 2 * D_FF // STAGE_ROWS

        def stage_copy(q):
            return pltpu.make_async_copy(wt_hbm.at[e, pl.ds(q * STAGE_ROWS, STAGE_ROWS)],
                                         stage_ref.at[q % 2], sem_ref.at[q % 2])

        down_copy = pltpu.make_async_copy(wd_hbm.at[e], wd_ref, sem_ref.at[2])
        down_copy.start()
        stage_copy(0).start()
        stage_copy(1).start()
        for q in range(n_stage):
            stage_copy(q).wait()
            w32 = pltpu.bitcast(stage_ref[q % 2], jnp.uint32)
            f0, f1 = q * STAGE_ROWS // 2, (q + 1) * STAGE_ROWS // 2
            wg_ref[f0:f1, :] = pltpu.bitcast(w32 << 16, F32).astype(BF16)
            wu_ref[f0:f1, :] = pltpu.bitcast(w32 & jnp.uint32(0xFFFF0000), F32).astype(BF16)
            if q + 2 < n_stage:
                stage_copy(q + 2).start()
        down_copy.wait()

    @pl.when(used)
    def _():
        x = xs_ref[...].astype(BF16)
        acc = jnp.zeros(ys_ref.shape, F32) + bd_ref[0]
        for f in range(D_FF // FF_TILE):
            lo, hi = f * FF_TILE, (f + 1) * FF_TILE
            gate = (lax.dot_general(x, wg_ref[lo:hi, :], nt, preferred_element_type=F32)
                    + bgu_ref[0, :, lo:hi])
            up = (lax.dot_general(x, wu_ref[lo:hi, :], nt, preferred_element_type=F32)
                  + bgu_ref[0, :, D_FF + lo:D_FF + hi])
            gate = jnp.minimum(gate, SWIGLU_LIMIT)
            up = jnp.clip(up, -SWIGLU_LIMIT, SWIGLU_LIMIT)
            act = (up + 1.0) * (gate * jax.nn.sigmoid(SWIGLU_ALPHA * gate))
            acc = acc + jnp.dot(act.astype(BF16), wd_ref[lo:hi, :], preferred_element_type=F32)
        ys_ref[...] = acc

    @pl.when(jnp.logical_not(used))
    def _():
        ys_ref[...] = jnp.zeros(ys_ref.shape, F32)


def _experts(block_e, n_used, xs, w_gu, b_gu, w_down, b_down):
    n_rows, d = xs.shape
    m = MOE_BLOCK
    n_blocks = n_rows // m
    last = lambda i, be, nu: jnp.minimum(i, nu[0] - 1)
    grid_spec = pltpu.PrefetchScalarGridSpec(
        num_scalar_prefetch=2,
        grid=(n_blocks,),
        in_specs=[
            pl.BlockSpec((m, d), lambda i, be, nu: (last(i, be, nu), 0)),
            pl.BlockSpec((1, 1, 2 * D_FF), lambda i, be, nu: (be[i], 0, 0)),
            pl.BlockSpec((1, 1, d), lambda i, be, nu: (be[i], 0, 0)),
            pl.BlockSpec(memory_space=pl.ANY),
            pl.BlockSpec(memory_space=pl.ANY),
        ],
        out_specs=pl.BlockSpec((m, d), lambda i, be, nu: (i, 0)),
        scratch_shapes=[pltpu.VMEM((2, STAGE_ROWS, d), BF16), pltpu.VMEM((D_FF, d), BF16),
                        pltpu.VMEM((D_FF, d), BF16), pltpu.VMEM((D_FF, d), BF16),
                        pltpu.SemaphoreType.DMA((3,))],
    )
    return pl.pallas_call(
        _experts_kernel,
        grid_spec=grid_spec,
        out_shape=jax.ShapeDtypeStruct((n_rows, d), F32),
        compiler_params=_params(("arbitrary",)),
        name="experts",
    )(block_e, n_used, xs, b_gu, b_down, w_gu, w_down)


def _combine_kernel(dest_ref, x1_ref, gate_ref, mod_ref, ys_hbm, o_ref, buf_ref, sem_ref):
    i = pl.program_id(0)
    n = pl.num_programs(0)
    tc = x1_ref.shape[0]

    def issue(step, slot):
        def body(r, c):
            for k in range(TOP_K):
                d = dest_ref[(step * tc + r) * TOP_K + k]
                pltpu.make_async_copy(ys_hbm.at[pl.ds(d, 1)], buf_ref.at[slot, k, pl.ds(r, 1)],
                                      sem_ref.at[slot]).start()
            return c
        lax.fori_loop(0, tc, body, 0)

    @pl.when(i == 0)
    def _():
        issue(0, 0)

    slot = i % 2

    @pl.when(i + 1 < n)
    def _():
        issue(i + 1, 1 - slot)

    for k in range(TOP_K):
        pltpu.make_async_copy(ys_hbm.at[pl.ds(0, tc)], buf_ref.at[slot, k], sem_ref.at[slot]).wait()
    g2 = mod_ref[0, 5:6, :]
    gate = gate_ref[...]
    acc = gate[:, 0:1] * buf_ref[slot, 0]
    for k in range(1, TOP_K):
        acc = acc + gate[:, k:k + 1] * buf_ref[slot, k]
    o_ref[...] = x1_ref[...] + g2 * acc


def _combine(dest, x1, gate, mod, ys, seq_len):
    t, d = x1.shape
    tc = min(COMBINE_TILE, seq_len)
    per_seq = seq_len // tc
    grid_spec = pltpu.PrefetchScalarGridSpec(
        num_scalar_prefetch=1,
        grid=(t // tc,),
        in_specs=[
            pl.BlockSpec((tc, d), lambda i, dr: (i, 0)),
            pl.BlockSpec((tc, TOP_K), lambda i, dr: (i, 0)),
            pl.BlockSpec((1, N_MOD, d), lambda i, dr: (i // per_seq, 0, 0)),
            pl.BlockSpec(memory_space=pl.ANY),
        ],
        out_specs=pl.BlockSpec((tc, d), lambda i, dr: (i, 0)),
        scratch_shapes=[pltpu.VMEM((2, TOP_K, tc, d), F32), pltpu.SemaphoreType.DMA((2,))],
    )
    return pl.pallas_call(
        _combine_kernel,
        grid_spec=grid_spec,
        out_shape=jax.ShapeDtypeStruct((t, d), F32),
        compiler_params=_params(("arbitrary",)),
        name="combine",
    )(dest, x1, gate, mod, ys)


def _moe(h2, logits, x1, mod, w_gu, b_gu, w_down, b_down, seq_len):
    t, d = h2.shape
    m, e = MOE_BLOCK, N_EXPERTS
    idx, gate, rank, cnt = _route(logits)
    counts = cnt[0].astype(I32)
    padded = ((counts + m - 1) // m) * m
    pad_ends = jnp.cumsum(padded)
    pad_starts = pad_ends - padded
    dest = (pad_starts[idx] + rank).reshape(-1).astype(I32)
    n_blocks = -(-(t * TOP_K) // m) + e
    n_rows = n_blocks * m
    block_start = jnp.arange(n_blocks, dtype=I32) * m
    block_e = jnp.minimum(jnp.sum(pad_ends[None, :] <= block_start[:, None], axis=1), e - 1).astype(I32)
    n_used = (pad_ends[-1:] // m).astype(I32)
    zpos = jnp.concatenate([jnp.where(padded > 0, pad_ends - m, -1), n_used,
                            jnp.sum(padded > 0, keepdims=True)]).astype(I32)
    xs = _dispatch(dest, zpos, h2, n_rows)
    ys = _experts(block_e, n_used, xs, w_gu, b_gu, w_down, b_down)
    return _combine(dest, x1, gate, mod, ys, seq_len)


def kernel(x, c, positions, w_ada, b_ada, norm_mix_g, w_in, ssm_A_re, ssm_A_im, ssm_B_re, ssm_B_im, ssm_C_re, ssm_C_im, ssm_D, ssm_log_dt, q_lat_g, kv_lat_g, w_uq, w_ukv, q_nope_g, q_rope_g, k_nope_g, k_rope_g, out_ssm_g, out_mla_g, w_out, norm_ffn_g, w_router, b_router, w_gate_up, b_gate_up, w_down, b_down):
    b, l, d = x.shape
    depth = w_ada.shape[0]
    row = lambda v: v.reshape(1, -1).astype(F32)

    inv_freq = ROPE_BASE ** (-jnp.arange(0, QK_ROPE, 2, dtype=F32) / QK_ROPE)
    ang = positions.astype(F32)[..., None] * inv_freq
    cos, sin = jnp.cos(ang), jnp.sin(ang)
    cos2 = jnp.concatenate([cos, cos, cos, cos], axis=-1)
    sin2 = jnp.concatenate([-sin, sin, -sin, sin], axis=-1)
    c_pad = jnp.zeros((8, d), F32).at[:b].set(c)
    lane_head = jnp.arange(N_HEADS * QK_ROPE) // QK_ROPE
    seg = (lane_head[:, None] == lane_head[None, :]).astype(BF16)

    for li in range(depth):
        mod = _ada(c_pad, w_ada[li], b_ada[li].reshape(1, -1))[:b].reshape(b, N_MOD, d)
        u, gs, ql, kvl, kr = _inproj(x, mod, row(norm_mix_g[li]), w_in[li].astype(BF16))

        tables = _ssm_tables(ssm_A_re[li], ssm_A_im[li], ssm_B_re[li], ssm_B_im[li], ssm_C_re[li],
                             ssm_C_im[li], ssm_D[li], ssm_log_dt[li])
        y_s5 = _s5(u, tables)

        wq = w_uq[li].reshape(Q_LORA, N_HEADS, QK_DIM)
        wq = jnp.concatenate([wq[:, :, :QK_NOPE].reshape(Q_LORA, -1),
                              wq[:, :, QK_NOPE:].reshape(Q_LORA, -1)], axis=1).astype(BF16)
        wkv = w_ukv[li].reshape(KV_LORA, N_HEADS, QK_NOPE + V_HEAD)
        wkv = jnp.concatenate([wkv[:, :, :QK_NOPE].reshape(KV_LORA, -1),
                               wkv[:, :, QK_NOPE:].reshape(KV_LORA, -1)], axis=1).astype(BF16)
        q, k, v = _mla_prep(ql, kvl, kr, cos2, sin2, row(q_lat_g[li]), row(kv_lat_g[li]),
                            row(q_nope_g[li]), row(jnp.tile(q_rope_g[li], N_HEADS)), row(k_nope_g[li]),
                            row(k_rope_g[li]), wq, wkv, seg)
        y_mla = _attention(q, k, v)

        x1, h2, logits = _postmix(y_s5, gs, y_mla, x, mod, row(out_ssm_g[li]), row(out_mla_g[li]),
                                  w_out[li].astype(BF16), row(norm_ffn_g[li]), w_router[li].astype(F32),
                                  row(b_router[li]))

        w_gu = jnp.swapaxes(w_gate_up[li], 1, 2).astype(BF16)
        b_gu = b_gate_up[li].reshape(N_EXPERTS, D_FF, 2).transpose(0, 2, 1)
        out = _moe(h2.reshape(b * l, d), logits.reshape(b * l, N_EXPERTS), x1.reshape(b * l, d), mod,
                   w_gu, b_gu.reshape(N_EXPERTS, 1, 2 * D_FF).astype(F32), w_down[li].astype(BF16),
                   b_down[li].reshape(N_EXPERTS, 1, d).astype(F32), l)
        x = out.reshape(b, l, d)
    return x
```
